```python
import jax, jax.numpy as jnp
from jax import lax
import numpy as np

D_MODEL = 4096
BATCH = 4
SEQ = 4096
DEPTH = 2

N_MIXERS = 2
SELF_WIDTH = 3 * D_MODEL // 4
HEAD_DIM = 128
N_SELF_HEADS = SELF_WIDTH // HEAD_DIM
MEM_LEN = 256
N_MEM_HEADS = 4
MEM_HEAD_DIM = (D_MODEL - SELF_WIDTH) // N_MEM_HEADS
MEM_WIDTH = N_MEM_HEADS * MEM_HEAD_DIM
Q_LORA_RANK = 1536
KV_LORA_RANK = 512
QK_NOPE_DIM = 128
QK_ROPE_DIM = 64
V_HEAD_DIM = SELF_WIDTH // N_SELF_HEADS
ROPE_THETA = 10000.0
D_FF = 4 * D_MODEL
Q_BLOCK = 128
NORM_EPS = 1e-6
FOX_IN = 3 * SELF_WIDTH + N_SELF_HEADS + MEM_WIDTH
MLA_IN = Q_LORA_RANK + KV_LORA_RANK + QK_ROPE_DIM + MEM_WIDTH
N_FOX_LAYERS = (DEPTH + N_MIXERS - 1) // N_MIXERS
N_MLA_LAYERS = DEPTH // N_MIXERS

kernel_name = "fox_mla_hybrid_memory_trunk"


def rms_norm(x, g):
    xf = x.astype(jnp.float32)
    y = xf * lax.rsqrt(jnp.mean(xf * xf, axis=-1, keepdims=True) + NORM_EPS)
    return (y * g.astype(jnp.float32)).astype(x.dtype)


def rope_tables(positions):
    inv_freq = 1.0 / (ROPE_THETA ** (jnp.arange(0, QK_ROPE_DIM, 2, dtype=jnp.float32) / QK_ROPE_DIM))
    ang = positions.astype(jnp.float32)[..., None] * inv_freq
    return jnp.cos(ang)[:, :, None, :], jnp.sin(ang)[:, :, None, :]


def apply_rope(x, cos, sin):
    xf = x.astype(jnp.float32)
    x1, x2 = jnp.split(xf, 2, axis=-1)
    return jnp.concatenate([x1 * cos - x2 * sin, x2 * cos + x1 * sin], axis=-1).astype(x.dtype)


def causal_block_attention(q, k, v, decay_cum=None):
    b, s, h, dk = q.shape
    nb = s // Q_BLOCK
    scale = dk ** -0.5
    q_blocks = jnp.moveaxis(q.reshape(b, nb, Q_BLOCK, h, dk), 1, 0)
    key_pos = jnp.arange(s)
    xs = [jnp.arange(nb), q_blocks]
    if decay_cum is not None:
        key_decay = jnp.transpose(decay_cum, (0, 2, 1))[:, :, None, :]
        xs.append(jnp.moveaxis(decay_cum.reshape(b, nb, Q_BLOCK, h), 1, 0))

    def one_block(args):
        blk, q_blk = args[0], args[1]
        scores = jnp.einsum('bqhd,bkhd->bhqk', q_blk, k).astype(jnp.float32) * scale
        if decay_cum is not None:
            query_decay = jnp.transpose(args[2], (0, 2, 1))[..., None]
            scores = scores + (query_decay - key_decay)
        query_pos = blk * Q_BLOCK + jnp.arange(Q_BLOCK)
        causal = key_pos[None, :] <= query_pos[:, None]
        scores = jnp.where(causal, scores, -jnp.inf)
        probs = jax.nn.softmax(scores, axis=-1).astype(v.dtype)
        return jnp.einsum('bhqk,bkhd->bqhd', probs, v)

    out = lax.map(one_block, tuple(xs))
    return jnp.moveaxis(out, 0, 1).reshape(b, s, h, v.shape[-1])


def memory_branch(q_mem, memq_norm_g, mem_k, mem_v):
    b, s, _ = q_mem.shape
    q = rms_norm(q_mem.reshape(b, s, N_MEM_HEADS, MEM_HEAD_DIM), memq_norm_g)
    scores = jnp.einsum('bqhd,bmhd->bhqm', q, mem_k).astype(jnp.float32) * (MEM_HEAD_DIM ** -0.5)
    probs = jax.nn.softmax(scores, axis=-1).astype(mem_v.dtype)
    return jnp.einsum('bhqm,bmhd->bqhd', probs, mem_v).reshape(b, s, MEM_WIDTH)


def fox_mixer(h, w_in, b_f, q_norm_g, k_norm_g, memq_norm_g, mem_k, mem_v):
    b, s, _ = h.shape
    proj = h @ w_in
    q, k, v, f_logit, q_mem = jnp.split(
        proj, [SELF_WIDTH, 2 * SELF_WIDTH, 3 * SELF_WIDTH, 3 * SELF_WIDTH + N_SELF_HEADS], axis=-1)
    q = rms_norm(q.reshape(b, s, N_SELF_HEADS, HEAD_DIM), q_norm_g)
    k = rms_norm(k.reshape(b, s, N_SELF_HEADS, HEAD_DIM), k_norm_g)
    v = v.reshape(b, s, N_SELF_HEADS, HEAD_DIM)
    log_f = jax.nn.log_sigmoid((f_logit + b_f).astype(jnp.float32))
    decay_cum = jnp.cumsum(log_f, axis=1)
    o_self = causal_block_attention(q, k, v, decay_cum).reshape(b, s, SELF_WIDTH)
    o_mem = memory_branch(q_mem, memq_norm_g, mem_k, mem_v)
    return jnp.concatenate([o_self, o_mem], axis=-1)


def mla_mixer(h, cos, sin, w_in, q_a_norm_g, w_q_b, kv_a_norm_g, w_kv_b, q_norm_g, k_norm_g,
              memq_norm_g, mem_k, mem_v):
    b, s, _ = h.shape
    proj = h @ w_in
    c_q, c_kv, k_rope, q_mem = jnp.split(
        proj, [Q_LORA_RANK, Q_LORA_RANK + KV_LORA_RANK, Q_LORA_RANK + KV_LORA_RANK + QK_ROPE_DIM], axis=-1)
    q = (rms_norm(c_q, q_a_norm_g) @ w_q_b).reshape(b, s, N_SELF_HEADS, QK_NOPE_DIM + QK_ROPE_DIM)
    kv = (rms_norm(c_kv, kv_a_norm_g) @ w_kv_b).reshape(b, s, N_SELF_HEADS, QK_NOPE_DIM + V_HEAD_DIM)
    k_nope, v = jnp.split(kv, [QK_NOPE_DIM], axis=-1)
    k_rope = jnp.broadcast_to(k_rope[:, :, None, :], (b, s, N_SELF_HEADS, QK_ROPE_DIM))
    k = jnp.concatenate([k_nope, k_rope], axis=-1)
    q = rms_norm(q, q_norm_g)
    k = rms_norm(k, k_norm_g)
    q = jnp.concatenate([q[..., :QK_NOPE_DIM], apply_rope(q[..., QK_NOPE_DIM:], cos, sin)], axis=-1)
    k = jnp.concatenate([k[..., :QK_NOPE_DIM], apply_rope(k[..., QK_NOPE_DIM:], cos, sin)], axis=-1)
    o_self = causal_block_attention(q, k, v).reshape(b, s, SELF_WIDTH)
    o_mem = memory_branch(q_mem, memq_norm_g, mem_k, mem_v)
    return jnp.concatenate([o_self, o_mem], axis=-1)


def _dense(key, shape, fan_in):
    return jax.random.normal(key, shape, jnp.float32) * (fan_in ** -0.5)


def _gain(key, shape):
    return 1.0 + 0.02 * jax.random.normal(key, shape, jnp.float32)


def setup_inputs(seed: int = 0) -> dict:
    key = jax.random.key(seed)
    ks = jax.random.split(key, 24)
    nf, nm = N_FOX_LAYERS, N_MLA_LAYERS
    positions = (jax.random.randint(ks[2], (BATCH, 1), 0, 1024, dtype=jnp.int32)
                 + jnp.arange(SEQ, dtype=jnp.int32)[None, :])
    return {
        "x": jax.random.normal(ks[0], (BATCH, SEQ, D_MODEL), jnp.float32),
        "mem": jax.random.normal(ks[1], (BATCH, MEM_LEN, D_MODEL), jnp.float32),
        "positions": positions,
        "mem_norm_g": _gain(ks[3], (D_MODEL,)),
        "w_mem_kv": _dense(ks[4], (D_MODEL, 2 * MEM_WIDTH), D_MODEL),
        "mem_k_norm_g": _gain(ks[5], (MEM_HEAD_DIM,)),
        "attn_norm_g": _gain(ks[6], (DEPTH, D_MODEL)),
        "memq_norm_g": _gain(ks[7], (DEPTH, MEM_HEAD_DIM)),
        "w_o": _dense(ks[8], (DEPTH, D_MODEL, D_MODEL), D_MODEL),
        "mlp_norm_g": _gain(ks[9], (DEPTH, D_MODEL)),
        "w_up": _dense(ks[10], (DEPTH, D_MODEL, D_FF), D_MODEL),
        "w_down": _dense(ks[11], (DEPTH, D_FF, D_MODEL), D_FF),
        "fox_w_in": _dense(ks[12], (nf, D_MODEL, FOX_IN), D_MODEL),
        "fox_b_f": 2.0 + 0.5 * jax.random.normal(ks[13], (nf, N_SELF_HEADS), jnp.float32),
        "fox_q_norm_g": _gain(ks[14], (nf, HEAD_DIM)),
        "fox_k_norm_g": _gain(ks[15], (nf, HEAD_DIM)),
        "mla_w_in": _dense(ks[16], (nm, D_MODEL, MLA_IN), D_MODEL),
        "mla_q_a_norm_g": _gain(ks[17], (nm, Q_LORA_RANK)),
        "mla_w_q_b": _dense(ks[18], (nm, Q_LORA_RANK, N_SELF_HEADS * (QK_NOPE_DIM + QK_ROPE_DIM)), Q_LORA_RANK),
        "mla_kv_a_norm_g": _gain(ks[19], (nm, KV_LORA_RANK)),
        "mla_w_kv_b": _dense(ks[20], (nm, KV_LORA_RANK, N_SELF_HEADS * (QK_NOPE_DIM + V_HEAD_DIM)), KV_LORA_RANK),
        "mla_q_norm_g": _gain(ks[21], (nm, QK_NOPE_DIM + QK_ROPE_DIM)),
        "mla_k_norm_g": _gain(ks[22], (nm, QK_NOPE_DIM + QK_ROPE_DIM)),
    }


def reference(x, mem, positions, mem_norm_g, w_mem_kv, mem_k_norm_g, attn_norm_g, memq_norm_g, w_o,
              mlp_norm_g, w_up, w_down, fox_w_in, fox_b_f, fox_q_norm_g, fox_k_norm_g, mla_w_in,
              mla_q_a_norm_g, mla_w_q_b, mla_kv_a_norm_g, mla_w_kv_b, mla_q_norm_g, mla_k_norm_g):
    b, m, _ = mem.shape
    mem_k, mem_v = jnp.split(rms_norm(mem, mem_norm_g) @ w_mem_kv, 2, axis=-1)
    mem_k = rms_norm(mem_k.reshape(b, m, N_MEM_HEADS, MEM_HEAD_DIM), mem_k_norm_g)
    mem_v = mem_v.reshape(b, m, N_MEM_HEADS, MEM_HEAD_DIM)
    cos, sin = rope_tables(positions)
    for layer in range(DEPTH):
        j = layer // N_MIXERS
        h = rms_norm(x, attn_norm_g[layer])
        if layer % N_MIXERS == 0:
            mixed = fox_mixer(h, fox_w_in[j], fox_b_f[j], fox_q_norm_g[j], fox_k_norm_g[j],
                              memq_norm_g[layer], mem_k, mem_v)
        else:
            mixed = mla_mixer(h, cos, sin, mla_w_in[j], mla_q_a_norm_g[j], mla_w_q_b[j],
                              mla_kv_a_norm_g[j], mla_w_kv_b[j], mla_q_norm_g[j], mla_k_norm_g[j],
                              memq_norm_g[layer], mem_k, mem_v)
        x = x + mixed @ w_o[layer]
        h = rms_norm(x, mlp_norm_g[layer])
        x = x + jnp.square(jax.nn.relu(h @ w_up[layer])) @ w_down[layer]
    return x
```

```python
import functools
import math

import jax
import jax.numpy as jnp
from jax import lax
from jax.experimental import pallas as pl
from jax.experimental.pallas import tpu as pltpu

NORM_EPS = 1e-6
ROPE_THETA = 10000.0
LOG2E = 1.4426950408889634
LANES = 128
HEAD_SLOT = 256
VMEM_LIMIT = 56 * 1024 * 1024
NEG_BIG = -1e30

F32 = jnp.float32
BF16 = jnp.bfloat16


def _tile(n, pref, mult=LANES):
    if n <= pref:
        return n
    t = (pref // mult) * mult
    while t >= mult:
        if n % t == 0:
            return t
        t -= mult
    return n


def _params(sem):
    return pltpu.CompilerParams(dimension_semantics=sem, vmem_limit_bytes=VMEM_LIMIT)


def _split3(x):
    p1 = x.astype(BF16)
    r1 = x - p1.astype(F32)
    p2 = r1.astype(BF16)
    r2 = r1 - p2.astype(F32)
    return p1, p2, r2.astype(BF16)


def _rms(y, n):
    ms = jnp.sum(y * y, axis=-1, keepdims=True) * (1.0 / n)
    return y * lax.rsqrt(ms + NORM_EPS)


def _rmsnorm_body(x_ref, g_ref, o_ref):
    x = x_ref[...]
    o_ref[...] = (_rms(x, x.shape[-1]) * g_ref[...]).astype(o_ref.dtype)


def _rmsnorm(x, g, tr=256):
    t, d = x.shape
    tr = _tile(t, tr, 8)
    return pl.pallas_call(
        _rmsnorm_body,
        out_shape=jax.ShapeDtypeStruct((t, d), BF16),
        grid=(t // tr,),
        in_specs=[pl.BlockSpec((tr, d), lambda i: (i, 0)),
                  pl.BlockSpec((1, d), lambda i: (0, 0))],
        out_specs=pl.BlockSpec((tr, d), lambda i: (i, 0)),
        compiler_params=_params(("parallel",)),
        name="rmsnorm",
    )(x, g.reshape(1, d))


def _mm_body(*refs, na, ne, no, nk, epi):
    a_refs = refs[:na]
    b_refs = refs[na:2 * na]
    e_refs = refs[2 * na:2 * na + ne]
    o_refs = refs[2 * na + ne:2 * na + ne + no]

    def compute():
        acc = None
        for a, b in zip(a_refs, b_refs):
            d = jnp.dot(a[...], b[...], preferred_element_type=F32)
            acc = d if acc is None else acc + d
        return acc

    if nk == 1:
        epi(compute(), e_refs, o_refs)
        return

    acc_ref = refs[-1]
    k = pl.program_id(2)

    @pl.when(k == 0)
    def _():
        acc_ref[...] = compute()

    @pl.when(k > 0)
    def _():
        acc_ref[...] += compute()

    @pl.when(k == nk - 1)
    def _():
        epi(acc_ref[...], e_refs, o_refs)


def _mm(name, a_parts, b_parts, epi, extras, outs, *, tm, tn, tk=None):
    t = a_parts[0].shape[0]
    n = b_parts[0].shape[1]
    assert t % tm == 0 and n % tn == 0
    if tk is None:
        nk = 1
        a_specs = [pl.BlockSpec((tm, a.shape[1]), lambda i, j, k: (i, 0)) for a in a_parts]
        b_specs = [pl.BlockSpec((b.shape[0], tn), lambda i, j, k: (0, j)) for b in b_parts]
    else:
        assert len(a_parts) == 1 and a_parts[0].shape[1] % tk == 0
        nk = a_parts[0].shape[1] // tk
        a_specs = [pl.BlockSpec((tm, tk), lambda i, j, k: (i, k))]
        b_specs = [pl.BlockSpec((tk, tn), lambda i, j, k: (k, j))]
    e_specs = [pl.BlockSpec(bs, im) for (_, bs, im) in extras]
    out_shape = [jax.ShapeDtypeStruct((t, cols), dt) for (cols, _, dt) in outs]
    out_specs = [pl.BlockSpec((tm, bc), lambda i, j, k: (i, j)) for (_, bc, _) in outs]
    scratch = [pltpu.VMEM((tm, tn), F32)] if nk > 1 else []
    body = functools.partial(_mm_body, na=len(a_parts), ne=len(extras), no=len(outs), nk=nk,
                             epi=epi)
    res = pl.pallas_call(
        body,
        out_shape=out_shape,
        grid=(t // tm, n // tn, nk),
        in_specs=a_specs + b_specs + e_specs,
        out_specs=out_specs,
        scratch_shapes=scratch,
        compiler_params=_params(("parallel", "parallel", "arbitrary")),
        name=name,
    )(*a_parts, *b_parts, *[e[0] for e in extras])
    return res


def _epi_plain(acc, e_refs, o_refs):
    o_refs[0][...] = acc.astype(o_refs[0].dtype)


def _epi_relu2(acc, e_refs, o_refs):
    r = jnp.maximum(acc, 0.0)
    o_refs[0][...] = (r * r).astype(o_refs[0].dtype)


def _epi_resid(acc, e_refs, o_refs):
    o_refs[0][...] = e_refs[0][...] + acc


def _epi_rownorm(acc, e_refs, o_refs):
    o_refs[0][...] = (_rms(acc, acc.shape[-1]) * e_refs[0][...]).astype(o_refs[0].dtype)


def _epi_ckv(acc, e_refs, o_refs, *, rank):
    y = acc[:, :rank]
    o_refs[0][...] = (_rms(y, rank) * e_refs[0][...]).astype(o_refs[0].dtype)
    o_refs[1][...] = acc[:, rank:]


def _epi_mem(acc, e_refs, o_refs, *, heads, hd):
    g_ref, mk_ref, mv_ref = e_refs
    c = (hd ** -0.5) * LOG2E
    for h in range(heads):
        sl = slice(h * hd, (h + 1) * hd)
        q = (_rms(acc[:, sl], hd) * (g_ref[...] * c)).astype(BF16)
        s = lax.dot_general(q, mk_ref[0, :, sl], (((1,), (1,)), ((), ())),
                            preferred_element_type=F32)
        p = jnp.exp2(s - jnp.max(s, axis=-1, keepdims=True))
        l = jnp.sum(p, axis=-1, keepdims=True)
        o = jnp.dot(p.astype(BF16), mv_ref[0, :, sl], preferred_element_type=F32)
        o_refs[0][:, sl] = (o / l).astype(o_refs[0].dtype)


def _epi_headnorm(acc, e_refs, o_refs, *, hd):
    for h in range(acc.shape[-1] // hd):
        sl = slice(h * hd, (h + 1) * hd)
        o_refs[0][:, sl] = (_rms(acc[:, sl], hd) * e_refs[0][...]).astype(o_refs[0].dtype)


def _decay_lanes(dcs_ref, head, sign_q):
    d = dcs_ref[...]
    pieces = jnp.concatenate(_split3(d), axis=-1)
    r = lax.broadcasted_iota(jnp.int32, (3 * LANES, LANES), 0)
    c = lax.broadcasted_iota(jnp.int32, (3 * LANES, LANES), 1)
    if sign_q:
        sel = jnp.where((c < 3) & (r == c * LANES + head), 1.0, 0.0)
    else:
        sel = jnp.where((c >= 3) & (c < 6) & (r == (c - 3) * LANES + head), -1.0, 0.0)
    aug = jnp.dot(pieces, sel.astype(BF16), preferred_element_type=F32)
    lane = lax.broadcasted_iota(jnp.int32, (1, LANES), 1)
    ones = jnp.where((lane >= 3) & (lane < 6), 1.0, 0.0) if sign_q else jnp.where(lane < 3, 1.0, 0.0)
    return aug + ones


def _epi_fox_qk(acc, e_refs, o_refs, *, hd, is_q):
    g_ref, dcs_ref = e_refs
    hpt = acc.shape[-1] // hd
    j = pl.program_id(1)
    c = (hd ** -0.5) * LOG2E if is_q else 1.0
    g = g_ref[...] * c
    for h in range(hpt):
        y = _rms(acc[:, h * hd:(h + 1) * hd], hd) * g
        o_refs[0][:, h * HEAD_SLOT:h * HEAD_SLOT + hd] = y.astype(o_refs[0].dtype)
        aug = _decay_lanes(dcs_ref, j * hpt + h, is_q)
        o_refs[0][:, h * HEAD_SLOT + hd:(h + 1) * HEAD_SLOT] = aug.astype(o_refs[0].dtype)


def _rope(u, c_ref, s1_ref, s2_ref, half):
    return (u * c_ref[...] + pltpu.roll(u, LANES - half, 1) * s1_ref[...]
            + pltpu.roll(u, half, 1) * s2_ref[...])


def _epi_mla_q(acc, e_refs, o_refs, *, nope, rope):
    g_ref, c_ref, s1_ref, s2_ref = e_refs
    qk = nope + rope
    c = (qk ** -0.5) * LOG2E
    g = g_ref[...] * c
    for h in range(acc.shape[-1] // HEAD_SLOT):
        y = _rms(acc[:, h * HEAD_SLOT:(h + 1) * HEAD_SLOT], qk) * g
        o_refs[0][:, h * HEAD_SLOT:h * HEAD_SLOT + nope] = y[:, :nope].astype(o_refs[0].dtype)
        u = _rope(y[:, nope:], c_ref, s1_ref, s2_ref, rope // 2)
        o_refs[0][:, h * HEAD_SLOT + nope:(h + 1) * HEAD_SLOT] = u.astype(o_refs[0].dtype)


def _epi_mla_k(acc, e_refs, o_refs, *, nope, rope):
    g_ref, kr_ref, c_ref, s1_ref, s2_ref = e_refs
    qk = nope + rope
    kr = kr_ref[...]
    ss_r = jnp.sum(kr * kr, axis=-1, keepdims=True)
    g = g_ref[...]
    for h in range(acc.shape[-1] // nope):
        y = acc[:, h * nope:(h + 1) * nope]
        ms = (jnp.sum(y * y, axis=-1, keepdims=True) + ss_r) * (1.0 / qk)
        r = lax.rsqrt(ms + NORM_EPS)
        o_refs[0][:, h * HEAD_SLOT:h * HEAD_SLOT + nope] = (y * r * g[:, :nope]).astype(
            o_refs[0].dtype)
        u = _rope(kr * r * g[:, nope:], c_ref, s1_ref, s2_ref, rope // 2)
        o_refs[0][:, h * HEAD_SLOT + nope:(h + 1) * HEAD_SLOT] = u.astype(o_refs[0].dtype)


def _gate_body(f_ref, b_ref, o_ref, *, chunk):
    s = f_ref.shape[0]
    r = lax.broadcasted_iota(jnp.int32, (chunk, chunk), 0)
    c = lax.broadcasted_iota(jnp.int32, (chunk, chunk), 1)
    tri = jnp.where(r >= c, 1.0, 0.0).astype(BF16)
    carry = jnp.zeros((1, f_ref.shape[1]), F32)
    for i in range(s // chunk):
        z = f_ref[i * chunk:(i + 1) * chunk, :] + b_ref[...]
        lf = jnp.minimum(z, 0.0) - jnp.log1p(jnp.exp(-jnp.abs(z)))
        p1, p2, p3 = _split3(lf)
        cum = (jnp.dot(tri, p1, preferred_element_type=F32)
               + jnp.dot(tri, p2, preferred_element_type=F32)
               + jnp.dot(tri, p3, preferred_element_type=F32)) + carry
        o_ref[i * chunk:(i + 1) * chunk, :] = cum * LOG2E
        carry = cum[chunk - 1:chunk, :]


def _fox_gates(f_logit, b_f_row, seq):
    t, w = f_logit.shape
    chunk = _tile(seq, 256, 8)
    return pl.pallas_call(
        functools.partial(_gate_body, chunk=chunk),
        out_shape=jax.ShapeDtypeStruct((t, w), F32),
        grid=(t // seq,),
        in_specs=[pl.BlockSpec((seq, w), lambda b: (b, 0)),
                  pl.BlockSpec((1, w), lambda b: (0, 0))],
        out_specs=pl.BlockSpec((seq, w), lambda b: (b, 0)),
        compiler_params=_params(("parallel",)),
        name="fox_gates",
    )(f_logit, b_f_row)


def _rope_body(pos_ref, inv_ref, c_ref, s1_ref, s2_ref, *, half):
    ang = pos_ref[...].astype(F32) * inv_ref[...]
    cosv = jnp.cos(ang)
    sinv = jnp.sin(ang)
    lane = lax.broadcasted_iota(jnp.int32, ang.shape, 1)
    c_ref[...] = jnp.where(lane < 2 * half, cosv, 0.0)
    s1_ref[...] = jnp.where(lane < half, -sinv, 0.0)
    s2_ref[...] = jnp.where((lane >= half) & (lane < 2 * half), sinv, 0.0)


def _rope_tables(positions, rope):
    t = positions.size
    half = rope // 2
    inv_freq = 1.0 / (ROPE_THETA ** (jnp.arange(0, rope, 2, dtype=F32) / rope))
    inv = jnp.zeros((1, LANES), F32).at[0, :half].set(inv_freq).at[0, half:rope].set(inv_freq)
    tr = _tile(t, 512, 8)
    spec = pl.BlockSpec((tr, LANES), lambda i: (i, 0))
    shp = jax.ShapeDtypeStruct((t, LANES), F32)
    return pl.pallas_call(
        functools.partial(_rope_body, half=half),
        out_shape=[shp, shp, shp],
        grid=(t // tr,),
        in_specs=[pl.BlockSpec((tr, 1), lambda i: (i, 0)),
                  pl.BlockSpec((1, LANES), lambda i: (0, 0))],
        out_specs=[spec, spec, spec],
        compiler_params=_params(("parallel",)),
        name="rope_tables",
    )(positions.reshape(t, 1), inv)


def _flash_body(q_ref, k_ref, v_ref, o_ref, *, tq):
    qi = pl.program_id(2)
    q = q_ref[...]
    dv = v_ref.shape[-1]

    def step(j, carry, masked):
        m, l, acc = carry
        start = pl.multiple_of(j * tq, tq)
        k = k_ref[pl.ds(start, tq), :]
        v = v_ref[pl.ds(start, tq), :]
        s = lax.dot_general(q, k, (((1,), (1,)), ((), ())), preferred_element_type=F32)
        if masked:
            r = lax.broadcasted_iota(jnp.int32, s.shape, 0)
            c = lax.broadcasted_iota(jnp.int32, s.shape, 1)
            s = jnp.where(r >= c, s, NEG_BIG)
        m_new = jnp.maximum(m, jnp.max(s, axis=-1, keepdims=True))
        alpha = jnp.exp2(m - m_new)
        p = jnp.exp2(s - m_new)
        l = alpha * l + jnp.sum(p, axis=-1, keepdims=True)
        acc = alpha * acc + jnp.dot(p.astype(BF16), v, preferred_element_type=F32)
        return m_new, l, acc

    init = (jnp.full((tq, 1), NEG_BIG, F32), jnp.zeros((tq, 1), F32), jnp.zeros((tq, dv), F32))
    carry = lax.fori_loop(0, qi, functools.partial(step, masked=False), init)
    _, l, acc = step(qi, carry, True)
    o_ref[...] = (acc / l).astype(o_ref.dtype)


def _flash(q, k, v, *, batch, seq, heads, tq=512):
    t = q.shape[0]
    dv = v.shape[1] // heads
    tq = _tile(seq, tq, LANES)
    nq = seq // tq
    return pl.pallas_call(
        functools.partial(_flash_body, tq=tq),
        out_shape=jax.ShapeDtypeStruct((t, heads * dv), BF16),
        grid=(batch, heads, nq),
        in_specs=[pl.BlockSpec((tq, HEAD_SLOT), lambda b, h, i: (b * nq + i, h)),
                  pl.BlockSpec((seq, HEAD_SLOT), lambda b, h, i: (b, h)),
                  pl.BlockSpec((seq, dv), lambda b, h, i: (b, h))],
        out_specs=pl.BlockSpec((tq, dv), lambda b, h, i: (b * nq + i, h)),
        compiler_params=_params(("parallel", "parallel", "arbitrary")),
        name="flash",
    )(q, k, v)


def _bf(w):
    return w.astype(BF16)


def _row(g):
    return g.reshape(1, -1).astype(F32)


def _const_spec(shape):
    return (shape, lambda i, j, k: (0,) * len(shape))


def _mem_branch(name, h, w_qmem, memq_g, mem_k, mem_v, *, seq, tm):
    width = w_qmem.shape[1]
    hd = memq_g.shape[0]
    mlen = mem_k.shape[1]
    per_b = seq // tm
    extras = [(_row(memq_g),) + _const_spec((1, hd)),
              (mem_k, (1, mlen, width), lambda i, j, k: (i // per_b, 0, 0)),
              (mem_v, (1, mlen, width), lambda i, j, k: (i // per_b, 0, 0))]
    epi = functools.partial(_epi_mem, heads=width // hd, hd=hd)
    return _mm(name, [h], [_bf(w_qmem)], epi, extras, [(width, width, BF16)], tm=tm, tn=width)[0]


def _out_and_mlp(x, o_self, o_mem, w_o, mlp_g, w_up, w_down, *, tm):
    t, d = x.shape
    sw = o_self.shape[1]
    tn = _tile(d, 512)
    x = _mm("attn_out", [o_self, o_mem], [_bf(w_o[:sw]), _bf(w_o[sw:])], _epi_resid,
            [(x, (tm, tn), lambda i, j, k: (i, j))], [(d, tn, F32)], tm=tm, tn=tn)[0]
    h = _rmsnorm(x, mlp_g)
    dff = w_up.shape[1]
    hid = _mm("mlp_up", [h], [_bf(w_up)], _epi_relu2, [], [(dff, _tile(dff, 1024), BF16)],
              tm=tm, tn=_tile(dff, 1024))[0]
    tn = _tile(d, 1024)
    x = _mm("mlp_down", [hid], [_bf(w_down)], _epi_resid,
            [(x, (tm, tn), lambda i, j, k: (i, j))], [(d, tn, F32)],
            tm=tm, tn=tn, tk=_tile(dff, 2048))[0]
    return x


def _fox_layer(x, h, w_in, b_f, q_g, k_g, memq_g, mem_k, mem_v, *, batch, seq, tm):
    t, d = x.shape
    heads = b_f.shape[0]
    hd = q_g.shape[0]
    sw = heads * hd
    assert hd == LANES
    w_q, w_k, w_v = (_bf(w_in[:, i * sw:(i + 1) * sw]) for i in range(3))
    w_f = jnp.zeros((d, LANES), BF16).at[:, :heads].set(_bf(w_in[:, 3 * sw:3 * sw + heads]))
    w_qmem = w_in[:, 3 * sw + heads:]

    f_logit = _mm("fox_f", [h], [w_f], _epi_plain, [], [(LANES, LANES, F32)], tm=tm, tn=LANES)[0]
    b_row = jnp.zeros((1, LANES), F32).at[0, :heads].set(b_f)
    dcs = _fox_gates(f_logit, b_row, seq)

    tn = _tile(sw, 1024)
    slot_cols = heads * HEAD_SLOT
    dcs_extra = (dcs, (tm, LANES), lambda i, j, k: (i, 0))
    qk = []
    for name, w, g, is_q in (("fox_q", w_q, q_g, True), ("fox_k", w_k, k_g, False)):
        epi = functools.partial(_epi_fox_qk, hd=hd, is_q=is_q)
        extras = [(_row(g),) + _const_spec((1, hd)), dcs_extra]
        qk.append(_mm(name, [h], [w], epi, extras,
                      [(slot_cols, tn // hd * HEAD_SLOT, BF16)], tm=tm, tn=tn)[0])
    v = _mm("fox_v", [h], [w_v], _epi_plain, [], [(sw, tn, BF16)], tm=tm, tn=tn)[0]
    o_self = _flash(qk[0], qk[1], v, batch=batch, seq=seq, heads=heads)
    o_mem = _mem_branch("fox_mem", h, w_qmem, memq_g, mem_k, mem_v, seq=seq, tm=tm)
    return o_self, o_mem


def _mla_layer(x, h, tables, w_in, q_a_g, w_q_b, kv_a_g, w_kv_b, q_g, k_g, memq_g, mem_k, mem_v,
               *, batch, seq, heads, tm):
    t, d = x.shape
    q_rank = q_a_g.shape[0]
    kv_rank = kv_a_g.shape[0]
    qk = q_g.shape[0]
    vd = (d - mem_k.shape[2]) // heads
    nope = w_kv_b.shape[1] // heads - vd
    rope = qk - nope
    assert nope == LANES and rope <= LANES and vd % LANES == 0
    c_tab, s1_tab, s2_tab = tables
    tab_extras = [(tab, (tm, LANES), lambda i, j, k: (i, 0)) for tab in (c_tab, s1_tab, s2_tab)]

    tm_q = _tile(seq, 512, 8)
    c_q = _mm("mla_cq", [h], [_bf(w_in[:, :q_rank])], _epi_rownorm,
              [(_row(q_a_g),) + _const_spec((1, q_rank))], [(q_rank, q_rank, BF16)],
              tm=tm_q, tn=q_rank)[0]
    w_ckv = jnp.zeros((d, kv_rank + LANES), BF16).at[:, :kv_rank + rope].set(
        _bf(w_in[:, q_rank:q_rank + kv_rank + rope]))
    c_kv, k_rope = _mm("mla_ckv", [h], [w_ckv], functools.partial(_epi_ckv, rank=kv_rank),
                       [(_row(kv_a_g),) + _const_spec((1, kv_rank))],
                       [(kv_rank, kv_rank, BF16), (LANES, LANES, F32)],
                       tm=tm, tn=kv_rank + LANES)
    w_qmem = w_in[:, q_rank + kv_rank + rope:]

    w_q = jnp.zeros((q_rank, heads, HEAD_SLOT), BF16).at[:, :, :qk].set(
        _bf(w_q_b).reshape(q_rank, heads, qk)).reshape(q_rank, heads * HEAD_SLOT)
    g_q = jnp.zeros((1, HEAD_SLOT), F32).at[0, :qk].set(q_g)
    tn = _tile(heads * HEAD_SLOT, 1024, HEAD_SLOT)
    q = _mm("mla_q", [c_q], [w_q], functools.partial(_epi_mla_q, nope=nope, rope=rope),
            [(g_q,) + _const_spec((1, HEAD_SLOT))] + tab_extras,
            [(heads * HEAD_SLOT, tn, BF16)], tm=tm, tn=tn)[0]

    w_kv = _bf(w_kv_b).reshape(kv_rank, heads, nope + vd)
    w_kn = w_kv[:, :, :nope].reshape(kv_rank, heads * nope)
    w_v = w_kv[:, :, nope:].reshape(kv_rank, heads * vd)
    g_k = jnp.zeros((1, nope + LANES), F32).at[0, :qk].set(k_g)
    tn = _tile(heads * nope, 1024)
    k = _mm("mla_k", [c_kv], [w_kn], functools.partial(_epi_mla_k, nope=nope, rope=rope),
            [(g_k,) + _const_spec((1, nope + LANES)),
             (k_rope, (tm, LANES), lambda i, j, k: (i, 0))] + tab_extras,
            [(heads * HEAD_SLOT, tn // nope * HEAD_SLOT, BF16)], tm=tm, tn=tn)[0]
    tn = _tile(heads * vd, 1024)
    v = _mm("mla_v", [c_kv], [w_v], _epi_plain, [], [(heads * vd, tn, BF16)], tm=tm, tn=tn)[0]

    o_self = _flash(q, k, v, batch=batch, seq=seq, heads=heads)
    o_mem = _mem_branch("mla_mem", h, w_qmem, memq_g, mem_k, mem_v, seq=seq, tm=tm)
    return o_self, o_mem


def kernel(x, mem, positions, mem_norm_g, w_mem_kv, mem_k_norm_g, attn_norm_g, memq_norm_g, w_o,
           mlp_norm_g, w_up, w_down, fox_w_in, fox_b_f, fox_q_norm_g, fox_k_norm_g, mla_w_in,
           mla_q_a_norm_g, mla_w_q_b, mla_kv_a_norm_g, mla_w_kv_b, mla_q_norm_g, mla_k_norm_g):
    batch, seq, d = x.shape
    mlen = mem.shape[1]
    mem_hd = mem_k_norm_g.shape[0]
    mem_w = w_mem_kv.shape[1] // 2
    heads = fox_b_f.shape[1]
    depth = attn_norm_g.shape[0]
    tm = _tile(seq, 1024, 8)

    mem_n = _rmsnorm(mem.reshape(batch * mlen, d), mem_norm_g)
    mem_k = _mm("mem_k", [mem_n], [_bf(w_mem_kv[:, :mem_w])],
                functools.partial(_epi_headnorm, hd=mem_hd),
                [(_row(mem_k_norm_g),) + _const_spec((1, mem_hd))], [(mem_w, mem_w, BF16)],
                tm=mlen, tn=mem_w)[0].reshape(batch, mlen, mem_w)
    mem_v = _mm("mem_v", [mem_n], [_bf(w_mem_kv[:, mem_w:])], _epi_plain, [],
                [(mem_w, mem_w, BF16)], tm=mlen, tn=mem_w)[0].reshape(batch, mlen, mem_w)

    rope = mla_q_norm_g.shape[1] - (mla_w_kv_b.shape[2] // heads - (d - mem_w) // heads)
    tables = _rope_tables(positions, rope)

    x = x.reshape(batch * seq, d)
    for layer in range(depth):
        j = layer // 2
        h = _rmsnorm(x, attn_norm_g[layer])
        if layer % 2 == 0:
            o_self, o_mem = _fox_layer(x, h, fox_w_in[j], fox_b_f[j], fox_q_norm_g[j],
                                       fox_k_norm_g[j], memq_norm_g[layer], mem_k, mem_v,
                                       batch=batch, seq=seq, tm=tm)
        else:
            o_self, o_mem = _mla_layer(x, h, tables, mla_w_in[j], mla_q_a_norm_g[j], mla_w_q_b[j],
                                       mla_kv_a_norm_g[j], mla_w_kv_b[j], mla_q_norm_g[j],
                                       mla_k_norm_g[j], memq_norm_g[layer], mem_k, mem_v,
                                       batch=batch, seq=seq, heads=heads, tm=tm)
        x = _out_and_mlp(x, o_self, o_mem, w_o[layer], mlp_norm_g[layer], w_up[layer],
                         w_down[layer], tm=tm)
    return x.reshape(batch, seq, d)
```

```python
import functools
import math

import jax
import jax.numpy as jnp
from jax import lax
from jax.experimental import pallas as pl
from jax.experimental.pallas import tpu as pltpu

NORM_EPS = 1e-6
ROPE_THETA = 10000.0
LOG2E = 1.4426950408889634
LANES = 128
HEAD_SLOT = 256
VMEM_LIMIT = 56 * 1024 * 1024
NEG_BIG = -1e30

F32 = jnp.float32
BF16 = jnp.bfloat16


def _tile(n, pref, mult=LANES):
    if n <= pref:
        return n
    t = (pref // mult) * mult
    while t >= mult:
        if n % t == 0:
            return t
        t -= mult
    return n


def _params(sem):
    return pltpu.CompilerParams(dimension_semantics=sem, vmem_limit_bytes=VMEM_LIMIT)


def _split3(x):
    p1 = x.astype(BF16)
    r1 = x - p1.astype(F32)
    p2 = r1.astype(BF16)
    r2 = r1 - p2.astype(F32)
    return p1, p2, r2.astype(BF16)


def _rms(y, n):
    ms = jnp.sum(y * y, axis=-1, keepdims=True) * (1.0 / n)
    return y * lax.rsqrt(ms + NORM_EPS)


def _rmsnorm_body(x_ref, g_ref, o_ref):
    x = x_ref[...]
    o_ref[...] = (_rms(x, x.shape[-1]) * g_ref[...]).astype(o_ref.dtype)


def _rmsnorm(x, g, tr=256):
    t, d = x.shape
    tr = _tile(t, tr, 8)
    return pl.pallas_call(
        _rmsnorm_body,
        out_shape=jax.ShapeDtypeStruct((t, d), BF16),
        grid=(t // tr,),
        in_specs=[pl.BlockSpec((tr, d), lambda i: (i, 0)),
                  pl.BlockSpec((1, d), lambda i: (0, 0))],
        out_specs=pl.BlockSpec((tr, d), lambda i: (i, 0)),
        compiler_params=_params(("parallel",)),
        name="rmsnorm",
    )(x, g.reshape(1, d))


def _mm_body(*refs, na, ne, no, nk, epi):
    a_refs = refs[:na]
    b_refs = refs[na:2 * na]
    e_refs = refs[2 * na:2 * na + ne]
    o_refs = refs[2 * na + ne:2 * na + ne + no]

    def compute():
        acc = None
        for a, b in zip(a_refs, b_refs):
            d = jnp.dot(a[...], b[...], preferred_element_type=F32)
            acc = d if acc is None else acc + d
        return acc

    if nk == 1:
        epi(compute(), e_refs, o_refs)
        return

    acc_ref = refs[-1]
    k = pl.program_id(2)

    @pl.when(k == 0)
    def _():
        acc_ref[...] = compute()

    @pl.when(k > 0)
    def _():
        acc_ref[...] += compute()

    @pl.when(k == nk - 1)
    def _():
        epi(acc_ref[...], e_refs, o_refs)


class _W:
    def __init__(self, arr, lead=None, r0=0, rows=None, c0=0, cols=None):
        self.arr, self.lead, self.r0, self.c0 = arr, lead, r0, c0
        self.rows = arr.shape[-2] - r0 if rows is None else rows
        self.cols = arr.shape[-1] - c0 if cols is None else cols

    def spec(self, br, bc):
        assert self.r0 % br == 0 and self.c0 % bc == 0 and self.rows % br == 0 and self.cols % bc == 0
        rb, cb, lead = self.r0 // br, self.c0 // bc, self.lead
        if lead is None:
            return pl.BlockSpec((br, bc), lambda i, j, k: (rb + k, cb + j))
        return pl.BlockSpec((None, br, bc), lambda i, j, k: (lead, rb + k, cb + j))


def _mm(name, a_parts, b_parts, epi, extras, outs, *, tm, tn, tk=None):
    b_parts = [b if isinstance(b, _W) else _W(b) for b in b_parts]
    t = a_parts[0].shape[0]
    n = b_parts[0].cols
    assert t % tm == 0 and n % tn == 0
    if tk is None:
        nk = 1
        a_specs = [pl.BlockSpec((tm, a.shape[1]), lambda i, j, k: (i, 0)) for a in a_parts]
        b_specs = [b.spec(b.rows, tn) for b in b_parts]
    else:
        assert len(a_parts) == 1 and a_parts[0].shape[1] % tk == 0
        nk = a_parts[0].shape[1] // tk
        a_specs = [pl.BlockSpec((tm, tk), lambda i, j, k: (i, k))]
        b_specs = [b_parts[0].spec(tk, tn)]
    e_specs = [pl.BlockSpec(bs, im) for (_, bs, im) in extras]
    out_shape = [jax.ShapeDtypeStruct((t, cols), dt) for (cols, _, dt) in outs]
    out_specs = [pl.BlockSpec((tm, bc), lambda i, j, k: (i, j)) for (_, bc, _) in outs]
    scratch = [pltpu.VMEM((tm, tn), F32)] if nk > 1 else []
    body = functools.partial(_mm_body, na=len(a_parts), ne=len(extras), no=len(outs), nk=nk,
                             epi=epi)
    res = pl.pallas_call(
        body,
        out_shape=out_shape,
        grid=(t // tm, n // tn, nk),
        in_specs=a_specs + b_specs + e_specs,
        out_specs=out_specs,
        scratch_shapes=scratch,
        compiler_params=_params(("parallel", "parallel", "arbitrary")),
        name=name,
    )(*a_parts, *[b.arr for b in b_parts], *[e[0] for e in extras])
    return res


def _epi_plain(acc, e_refs, o_refs):
    o_refs[0][...] = acc.astype(o_refs[0].dtype)


def _epi_relu2(acc, e_refs, o_refs):
    r = jnp.maximum(acc, 0.0)
    o_refs[0][...] = (r * r).astype(o_refs[0].dtype)


def _epi_resid(acc, e_refs, o_refs):
    o_refs[0][...] = e_refs[0][...] + acc


def _epi_rownorm(acc, e_refs, o_refs):
    o_refs[0][...] = (_rms(acc, acc.shape[-1]) * e_refs[0][...]).astype(o_refs[0].dtype)


def _epi_ckv(acc, e_refs, o_refs, *, rank):
    y = acc[:, :rank]
    o_refs[0][...] = (_rms(y, rank) * e_refs[0][...]).astype(o_refs[0].dtype)
    o_refs[1][...] = acc[:, rank:]


def _epi_mem(acc, e_refs, o_refs, *, heads, hd):
    g_ref, mk_ref, mv_ref = e_refs
    c = (hd ** -0.5) * LOG2E
    for h in range(heads):
        sl = slice(h * hd, (h + 1) * hd)
        q = (_rms(acc[:, sl], hd) * (g_ref[...] * c)).astype(BF16)
        s = lax.dot_general(q, mk_ref[0, :, sl], (((1,), (1,)), ((), ())),
                            preferred_element_type=F32)
        p = jnp.exp2(s - jnp.max(s, axis=-1, keepdims=True))
        l = jnp.sum(p, axis=-1, keepdims=True)
        o = jnp.dot(p.astype(BF16), mv_ref[0, :, sl], preferred_element_type=F32)
        o_refs[0][:, sl] = (o / l).astype(o_refs[0].dtype)


def _epi_headnorm(acc, e_refs, o_refs, *, hd):
    for h in range(acc.shape[-1] // hd):
        sl = slice(h * hd, (h + 1) * hd)
        o_refs[0][:, sl] = (_rms(acc[:, sl], hd) * e_refs[0][...]).astype(o_refs[0].dtype)


def _decay_lanes(dcs_ref, head, sign_q):
    d = dcs_ref[...]
    pieces = jnp.concatenate(_split3(d), axis=-1)
    r = lax.broadcasted_iota(jnp.int32, (3 * LANES, LANES), 0)
    c = lax.broadcasted_iota(jnp.int32, (3 * LANES, LANES), 1)
    if sign_q:
        sel = jnp.where((c < 3) & (r == c * LANES + head), 1.0, 0.0)
    else:
        sel = jnp.where((c >= 3) & (c < 6) & (r == (c - 3) * LANES + head), -1.0, 0.0)
    aug = jnp.dot(pieces, sel.astype(BF16), preferred_element_type=F32)
    lane = lax.broadcasted_iota(jnp.int32, (1, LANES), 1)
    ones = jnp.where((lane >= 3) & (lane < 6), 1.0, 0.0) if sign_q else jnp.where(lane < 3, 1.0, 0.0)
    return aug + ones


def _epi_fox_qk(acc, e_refs, o_refs, *, hd, is_q):
    g_ref, dcs_ref = e_refs
    hpt = acc.shape[-1] // hd
    j = pl.program_id(1)
    c = (hd ** -0.5) * LOG2E if is_q else 1.0
    g = g_ref[...] * c
    for h in range(hpt):
        y = _rms(acc[:, h * hd:(h + 1) * hd], hd) * g
        o_refs[0][:, h * HEAD_SLOT:h * HEAD_SLOT + hd] = y.astype(o_refs[0].dtype)
        aug = _decay_lanes(dcs_ref, j * hpt + h, is_q)
        o_refs[0][:, h * HEAD_SLOT + hd:(h + 1) * HEAD_SLOT] = aug.astype(o_refs[0].dtype)


def _rope(u, c_ref, s1_ref, s2_ref, half):
    return (u * c_ref[...] + pltpu.roll(u, LANES - half, 1) * s1_ref[...]
            + pltpu.roll(u, half, 1) * s2_ref[...])


def _epi_mla_q(acc, e_refs, o_refs, *, nope, rope):
    g_ref, c_ref, s1_ref, s2_ref = e_refs
    qk = nope + rope
    c = (qk ** -0.5) * LOG2E
    g = g_ref[...] * c
    for h in range(acc.shape[-1] // HEAD_SLOT):
        y = _rms(acc[:, h * HEAD_SLOT:(h + 1) * HEAD_SLOT], qk) * g
        o_refs[0][:, h * HEAD_SLOT:h * HEAD_SLOT + nope] = y[:, :nope].astype(o_refs[0].dtype)
        u = _rope(y[:, nope:], c_ref, s1_ref, s2_ref, rope // 2)
        o_refs[0][:, h * HEAD_SLOT + nope:(h + 1) * HEAD_SLOT] = u.astype(o_refs[0].dtype)


def _epi_mla_k(acc, e_refs, o_refs, *, nope, rope):
    g_ref, kr_ref, c_ref, s1_ref, s2_ref = e_refs
    qk = nope + rope
    kr = kr_ref[...]
    ss_r = jnp.sum(kr * kr, axis=-1, keepdims=True)
    g = g_ref[...]
    for h in range(acc.shape[-1] // nope):
        y = acc[:, h * nope:(h + 1) * nope]
        ms = (jnp.sum(y * y, axis=-1, keepdims=True) + ss_r) * (1.0 / qk)
        r = lax.rsqrt(ms + NORM_EPS)
        o_refs[0][:, h * HEAD_SLOT:h * HEAD_SLOT + nope] = (y * r * g[:, :nope]).astype(
            o_refs[0].dtype)
        u = _rope(kr * r * g[:, nope:], c_ref, s1_ref, s2_ref, rope // 2)
        o_refs[0][:, h * HEAD_SLOT + nope:(h + 1) * HEAD_SLOT] = u.astype(o_refs[0].dtype)


def _gate_body(f_ref, b_ref, o_ref, *, chunk):
    s = f_ref.shape[0]
    r = lax.broadcasted_iota(jnp.int32, (chunk, chunk), 0)
    c = lax.broadcasted_iota(jnp.int32, (chunk, chunk), 1)
    tri = jnp.where(r >= c, 1.0, 0.0).astype(BF16)
    carry = jnp.zeros((1, f_ref.shape[1]), F32)
    for i in range(s // chunk):
        z = f_ref[i * chunk:(i + 1) * chunk, :] + b_ref[...]
        lf = jnp.minimum(z, 0.0) - jnp.log1p(jnp.exp(-jnp.abs(z)))
        p1, p2, p3 = _split3(lf)
        cum = (jnp.dot(tri, p1, preferred_element_type=F32)
               + jnp.dot(tri, p2, preferred_element_type=F32)
               + jnp.dot(tri, p3, preferred_element_type=F32)) + carry
        o_ref[i * chunk:(i + 1) * chunk, :] = cum * LOG2E
        carry = cum[chunk - 1:chunk, :]


def _fox_gates(f_logit, b_f_row, seq):
    t, w = f_logit.shape
    chunk = _tile(seq, 256, 8)
    return pl.pallas_call(
        functools.partial(_gate_body, chunk=chunk),
        out_shape=jax.ShapeDtypeStruct((t, w), F32),
        grid=(t // seq,),
        in_specs=[pl.BlockSpec((seq, w), lambda b: (b, 0)),
                  pl.BlockSpec((1, w), lambda b: (0, 0))],
        out_specs=pl.BlockSpec((seq, w), lambda b: (b, 0)),
        compiler_params=_params(("parallel",)),
        name="fox_gates",
    )(f_logit, b_f_row)


def _rope_body(pos_ref, inv_ref, c_ref, s1_ref, s2_ref, *, half):
    ang = pos_ref[...].astype(F32) * inv_ref[...]
    cosv = jnp.cos(ang)
    sinv = jnp.sin(ang)
    lane = lax.broadcasted_iota(jnp.int32, ang.shape, 1)
    c_ref[...] = jnp.where(lane < 2 * half, cosv, 0.0)
    s1_ref[...] = jnp.where(lane < half, -sinv, 0.0)
    s2_ref[...] = jnp.where((lane >= half) & (lane < 2 * half), sinv, 0.0)


def _rope_tables(positions, rope):
    t = positions.size
    half = rope // 2
    inv_freq = 1.0 / (ROPE_THETA ** (jnp.arange(0, rope, 2, dtype=F32) / rope))
    inv = jnp.zeros((1, LANES), F32).at[0, :half].set(inv_freq).at[0, half:rope].set(inv_freq)
    tr = _tile(t, 512, 8)
    spec = pl.BlockSpec((tr, LANES), lambda i: (i, 0))
    shp = jax.ShapeDtypeStruct((t, LANES), F32)
    return pl.pallas_call(
        functools.partial(_rope_body, half=half),
        out_shape=[shp, shp, shp],
        grid=(t // tr,),
        in_specs=[pl.BlockSpec((tr, 1), lambda i: (i, 0)),
                  pl.BlockSpec((1, LANES), lambda i: (0, 0))],
        out_specs=[spec, spec, spec],
        compiler_params=_params(("parallel",)),
        name="rope_tables",
    )(positions.reshape(t, 1), inv)


def _flash_body(q_ref, k_ref, v_ref, o_ref, st_ref, *, tq, hp, dk, dv):
    qi = pl.program_id(2)

    def scores(h, j):
        start = pl.multiple_of(j * tq, tq)
        q = q_ref[:, h * dk:(h + 1) * dk]
        k = k_ref[pl.ds(start, tq), h * dk:(h + 1) * dk]
        st_ref[h] = lax.dot_general(k, q, (((1,), (1,)), ((), ())), preferred_element_type=F32)

    def step(j, stats, masked):
        start = pl.multiple_of(j * tq, tq)
        new = []
        for h in range(hp):
            m, l, acc = stats[h]
            if h + 1 < hp:
                scores(h + 1, j)
            elif not masked:
                scores(0, j + 1)
            st = st_ref[h]
            v = v_ref[pl.ds(start, tq), h * dv:(h + 1) * dv]
            if masked:
                key = lax.broadcasted_iota(jnp.int32, st.shape, 0)
                qry = lax.broadcasted_iota(jnp.int32, st.shape, 1)
                st = jnp.where(key <= qry, st, NEG_BIG)
            m_new = jnp.maximum(m, jnp.max(st, axis=0, keepdims=True))
            alpha = jnp.exp2(m - m_new)
            pt = jnp.exp2(st - m_new)
            l = alpha * l + jnp.sum(pt, axis=0, keepdims=True)
            pv = lax.dot_general(v, pt.astype(BF16), (((0,), (0,)), ((), ())),
                                 preferred_element_type=F32)
            new.append((m_new, l, alpha * acc + pv))
        return tuple(new)

    init = tuple((jnp.full((1, tq), NEG_BIG, F32), jnp.zeros((1, tq), F32),
                  jnp.zeros((dv, tq), F32)) for _ in range(hp))
    scores(0, 0)
    stats = lax.fori_loop(0, qi, functools.partial(step, masked=False), init)
    stats = step(qi, stats, True)
    for h in range(hp):
        _, l, acc = stats[h]
        o_ref[:, h * dv:(h + 1) * dv] = (acc / l).T.astype(o_ref.dtype)


def _flash(q, k, v, *, batch, seq, heads, tq=512, hp=6):
    t = q.shape[0]
    dv = v.shape[1] // heads
    tq = _tile(seq, tq, LANES)
    nq = seq // tq
    assert heads % hp == 0
    return pl.pallas_call(
        functools.partial(_flash_body, tq=tq, hp=hp, dk=HEAD_SLOT, dv=dv),
        out_shape=jax.ShapeDtypeStruct((t, heads * dv), BF16),
        grid=(batch, heads // hp, nq),
        in_specs=[pl.BlockSpec((tq, hp * HEAD_SLOT), lambda b, h, i: (b * nq + i, h)),
                  pl.BlockSpec((seq, hp * HEAD_SLOT), lambda b, h, i: (b, h)),
                  pl.BlockSpec((seq, hp * dv), lambda b, h, i: (b, h))],
        out_specs=pl.BlockSpec((tq, hp * dv), lambda b, h, i: (b * nq + i, h)),
        scratch_shapes=[pltpu.VMEM((hp, tq, tq), F32)],
        compiler_params=_params(("parallel", "parallel", "arbitrary")),
        name="flash",
    )(q, k, v)


def _bf(w):
    return w.astype(BF16)


def _row(g):
    return g.reshape(1, -1).astype(F32)


def _const_spec(shape):
    return (shape, lambda i, j, k: (0,) * len(shape))


def _mem_branch(name, h, w_qmem, memq_g, mem_k, mem_v, *, seq, tm):
    width = w_qmem.shape[1]
    hd = memq_g.shape[0]
    mlen = mem_k.shape[1]
    per_b = seq // tm
    extras = [(_row(memq_g),) + _const_spec((1, hd)),
              (mem_k, (1, mlen, width), lambda i, j, k: (i // per_b, 0, 0)),
              (mem_v, (1, mlen, width), lambda i, j, k: (i // per_b, 0, 0))]
    epi = functools.partial(_epi_mem, heads=width // hd, hd=hd)
    return _mm(name, [h], [_bf(w_qmem)], epi, extras, [(width, width, BF16)], tm=tm, tn=width)[0]


def _out_and_mlp(x, o_self, o_mem, layer, w_o, mlp_g, w_up, w_down, *, tm):
    t, d = x.shape
    sw = o_self.shape[1]
    tn = _tile(d, 512)
    x = _mm("attn_out", [o_self, o_mem],
            [_W(w_o, layer, 0, sw), _W(w_o, layer, sw, d - sw)], _epi_resid,
            [(x, (tm, tn), lambda i, j, k: (i, j))], [(d, tn, F32)], tm=tm, tn=tn)[0]
    h = _rmsnorm(x, mlp_g)
    dff = w_up.shape[-1]
    hid = _mm("mlp_up", [h], [_W(w_up, layer)], _epi_relu2, [], [(dff, _tile(dff, 1024), BF16)],
              tm=tm, tn=_tile(dff, 1024))[0]
    tn = _tile(d, 1024)
    x = _mm("mlp_down", [hid], [_W(w_down, layer)], _epi_resid,
            [(x, (tm, tn), lambda i, j, k: (i, j))], [(d, tn, F32)],
            tm=tm, tn=tn, tk=_tile(dff, 2048))[0]
    return x


def _fox_layer(x, h, w_in_all, j, b_f, q_g, k_g, memq_g, mem_k, mem_v, *, batch, seq, tm):
    t, d = x.shape
    heads = b_f.shape[0]
    hd = q_g.shape[0]
    sw = heads * hd
    assert hd == LANES
    w_q, w_k, w_v = (_W(w_in_all, j, 0, d, i * sw, sw) for i in range(3))
    w_in = w_in_all[j]
    w_f = jnp.zeros((d, LANES), BF16).at[:, :heads].set(w_in[:, 3 * sw:3 * sw + heads])
    w_qmem = w_in[:, 3 * sw + heads:]

    f_logit = _mm("fox_f", [h], [w_f], _epi_plain, [], [(LANES, LANES, F32)], tm=tm, tn=LANES)[0]
    b_row = jnp.zeros((1, LANES), F32).at[0, :heads].set(b_f)
    dcs = _fox_gates(f_logit, b_row, seq)

    tn = _tile(sw, 1024)
    slot_cols = heads * HEAD_SLOT
    dcs_extra = (dcs, (tm, LANES), lambda i, j, k: (i, 0))
    qk = []
    for name, w, g, is_q in (("fox_q", w_q, q_g, True), ("fox_k", w_k, k_g, False)):
        epi = functools.partial(_epi_fox_qk, hd=hd, is_q=is_q)
        extras = [(_row(g),) + _const_spec((1, hd)), dcs_extra]
        qk.append(_mm(name, [h], [w], epi, extras,
                      [(slot_cols, tn // hd * HEAD_SLOT, BF16)], tm=tm, tn=tn)[0])
    v = _mm("fox_v", [h], [w_v], _epi_plain, [], [(sw, tn, BF16)], tm=tm, tn=tn)[0]
    o_self = _flash(qk[0], qk[1], v, batch=batch, seq=seq, heads=heads)
    o_mem = _mem_branch("fox_mem", h, w_qmem, memq_g, mem_k, mem_v, seq=seq, tm=tm)
    return o_self, o_mem


def _mla_layer(x, h, tables, w_in_all, j, q_a_g, w_q_b, kv_a_g, w_kv_b, q_g, k_g, memq_g, mem_k,
               mem_v, *, batch, seq, heads, tm):
    t, d = x.shape
    w_in = w_in_all[j]
    q_rank = q_a_g.shape[0]
    kv_rank = kv_a_g.shape[0]
    qk = q_g.shape[0]
    vd = (d - mem_k.shape[2]) // heads
    nope = w_kv_b.shape[1] // heads - vd
    rope = qk - nope
    assert nope == LANES and rope <= LANES and vd % LANES == 0
    c_tab, s1_tab, s2_tab = tables
    tab_extras = [(tab, (tm, LANES), lambda i, j, k: (i, 0)) for tab in (c_tab, s1_tab, s2_tab)]

    tm_q = _tile(seq, 512, 8)
    c_q = _mm("mla_cq", [h], [_W(w_in_all, j, 0, d, 0, q_rank)], _epi_rownorm,
              [(_row(q_a_g),) + _const_spec((1, q_rank))], [(q_rank, q_rank, BF16)],
              tm=tm_q, tn=q_rank)[0]
    w_ckv = jnp.zeros((d, kv_rank + LANES), BF16).at[:, :kv_rank + rope].set(
        _bf(w_in[:, q_rank:q_rank + kv_rank + rope]))
    c_kv, k_rope = _mm("mla_ckv", [h], [w_ckv], functools.partial(_epi_ckv, rank=kv_rank),
                       [(_row(kv_a_g),) + _const_spec((1, kv_rank))],
                       [(kv_rank, kv_rank, BF16), (LANES, LANES, F32)],
                       tm=tm, tn=kv_rank + LANES)
    w_qmem = w_in[:, q_rank + kv_rank + rope:]

    w_q = jnp.zeros((q_rank, heads, HEAD_SLOT), BF16).at[:, :, :qk].set(
        _bf(w_q_b).reshape(q_rank, heads, qk)).reshape(q_rank, heads * HEAD_SLOT)
    g_q = jnp.zeros((1, HEAD_SLOT), F32).at[0, :qk].set(q_g)
    tn = _tile(heads * HEAD_SLOT, 1024, HEAD_SLOT)
    q = _mm("mla_q", [c_q], [w_q], functools.partial(_epi_mla_q, nope=nope, rope=rope),
            [(g_q,) + _const_spec((1, HEAD_SLOT))] + tab_extras,
            [(heads * HEAD_SLOT, tn, BF16)], tm=tm, tn=tn)[0]

    w_kv = _bf(w_kv_b).reshape(kv_rank, heads, nope + vd)
    w_kn = w_kv[:, :, :nope].reshape(kv_rank, heads * nope)
    w_v = w_kv[:, :, nope:].reshape(kv_rank, heads * vd)
    g_k = jnp.zeros((1, nope + LANES), F32).at[0, :qk].set(k_g)
    tn = _tile(heads * nope, 1024)
    k = _mm("mla_k", [c_kv], [w_kn], functools.partial(_epi_mla_k, nope=nope, rope=rope),
            [(g_k,) + _const_spec((1, nope + LANES)),
             (k_rope, (tm, LANES), lambda i, j, k: (i, 0))] + tab_extras,
            [(heads * HEAD_SLOT, tn // nope * HEAD_SLOT, BF16)], tm=tm, tn=tn)[0]
    tn = _tile(heads * vd, 1024)
    v = _mm("mla_v", [c_kv], [w_v], _epi_plain, [], [(heads * vd, tn, BF16)], tm=tm, tn=tn)[0]

    o_self = _flash(q, k, v, batch=batch, seq=seq, heads=heads)
    o_mem = _mem_branch("mla_mem", h, w_qmem, memq_g, mem_k, mem_v, seq=seq, tm=tm)
    return o_self, o_mem


def kernel(x, mem, positions, mem_norm_g, w_mem_kv, mem_k_norm_g, attn_norm_g, memq_norm_g, w_o,
           mlp_norm_g, w_up, w_down, fox_w_in, fox_b_f, fox_q_norm_g, fox_k_norm_g, mla_w_in,
           mla_q_a_norm_g, mla_w_q_b, mla_kv_a_norm_g, mla_w_kv_b, mla_q_norm_g, mla_k_norm_g):
    batch, seq, d = x.shape
    mlen = mem.shape[1]
    mem_hd = mem_k_norm_g.shape[0]
    mem_w = w_mem_kv.shape[1] // 2
    heads = fox_b_f.shape[1]
    depth = attn_norm_g.shape[0]
    tm = _tile(seq, 1024, 8)

    mem_n = _rmsnorm(mem.reshape(batch * mlen, d), mem_norm_g)
    mem_k = _mm("mem_k", [mem_n], [_bf(w_mem_kv[:, :mem_w])],
                functools.partial(_epi_headnorm, hd=mem_hd),
                [(_row(mem_k_norm_g),) + _const_spec((1, mem_hd))], [(mem_w, mem_w, BF16)],
                tm=mlen, tn=mem_w)[0].reshape(batch, mlen, mem_w)
    mem_v = _mm("mem_v", [mem_n], [_bf(w_mem_kv[:, mem_w:])], _epi_plain, [],
                [(mem_w, mem_w, BF16)], tm=mlen, tn=mem_w)[0].reshape(batch, mlen, mem_w)

    rope = mla_q_norm_g.shape[1] - (mla_w_kv_b.shape[2] // heads - (d - mem_w) // heads)
    tables = _rope_tables(positions, rope)

    fox_w, mla_w, w_o, w_up, w_down = (_bf(w) for w in (fox_w_in, mla_w_in, w_o, w_up, w_down))
    x = x.reshape(batch * seq, d)
    for layer in range(depth):
        j = layer // 2
        h = _rmsnorm(x, attn_norm_g[layer])
        if layer % 2 == 0:
            o_self, o_mem = _fox_layer(x, h, fox_w, j, fox_b_f[j], fox_q_norm_g[j],
                                       fox_k_norm_g[j], memq_norm_g[layer], mem_k, mem_v,
                                       batch=batch, seq=seq, tm=tm)
        else:
            o_self, o_mem = _mla_layer(x, h, tables, mla_w, j, mla_q_a_norm_g[j], mla_w_q_b[j],
                                       mla_kv_a_norm_g[j], mla_w_kv_b[j], mla_q_norm_g[j],
                                       mla_k_norm_g[j], memq_norm_g[layer], mem_k, mem_v,
                                       batch=batch, seq=seq, heads=heads, tm=tm)
        x = _out_and_mlp(x, o_self, o_mem, layer, w_o, mlp_norm_g[layer], w_up, w_down, tm=tm)
    return x.reshape(batch, seq, d)
```

```python
import functools
import math

import jax
import jax.numpy as jnp
from jax import lax
from jax.experimental import pallas as pl
from jax.experimental.pallas import tpu as pltpu

NORM_EPS = 1e-6
ROPE_THETA = 10000.0
LOG2E = 1.4426950408889634
LANES = 128
HEAD_SLOT = 256
VMEM_LIMIT = 56 * 1024 * 1024
NEG_BIG = -1e30

F32 = jnp.float32
BF16 = jnp.bfloat16


def _tile(n, pref, mult=LANES):
    if n <= pref:
        return n
    t = (pref // mult) * mult
    while t >= mult:
        if n % t == 0:
            return t
        t -= mult
    return n


def _params(sem):
    return pltpu.CompilerParams(dimension_semantics=sem, vmem_limit_bytes=VMEM_LIMIT)


def _split3(x):
    p1 = x.astype(BF16)
    r1 = x - p1.astype(F32)
    p2 = r1.astype(BF16)
    r2 = r1 - p2.astype(F32)
    return p1, p2, r2.astype(BF16)


def _rms(y, n):
    ms = jnp.sum(y * y, axis=-1, keepdims=True) * (1.0 / n)
    return y * lax.rsqrt(ms + NORM_EPS)


def _rmsnorm_body(x_ref, g_ref, o_ref):
    x = x_ref[...]
    o_ref[...] = (_rms(x, x.shape[-1]) * g_ref[...]).astype(o_ref.dtype)


def _rmsnorm(x, g, tr=256):
    t, d = x.shape
    tr = _tile(t, tr, 8)
    return pl.pallas_call(
        _rmsnorm_body,
        out_shape=jax.ShapeDtypeStruct((t, d), BF16),
        grid=(t // tr,),
        in_specs=[pl.BlockSpec((tr, d), lambda i: (i, 0)),
                  pl.BlockSpec((1, d), lambda i: (0, 0))],
        out_specs=pl.BlockSpec((tr, d), lambda i: (i, 0)),
        compiler_params=_params(("parallel",)),
        name="rmsnorm",
    )(x, g.reshape(1, d))


def _mm_body(*refs, na, ne, no, nk, epi):
    a_refs = refs[:na]
    b_refs = refs[na:2 * na]
    e_refs = refs[2 * na:2 * na + ne]
    o_refs = refs[2 * na + ne:2 * na + ne + no]

    def compute():
        acc = None
        for a, b in zip(a_refs, b_refs):
            d = jnp.dot(a[...], b[...], preferred_element_type=F32)
            acc = d if acc is None else acc + d
        return acc

    if nk == 1:
        epi(compute(), e_refs, o_refs)
        return

    acc_ref = refs[-1]
    k = pl.program_id(2)

    @pl.when(k == 0)
    def _():
        acc_ref[...] = compute()

    @pl.when(k > 0)
    def _():
        acc_ref[...] += compute()

    @pl.when(k == nk - 1)
    def _():
        epi(acc_ref[...], e_refs, o_refs)


class _W:
    def __init__(self, arr, lead=None, r0=0, rows=None, c0=0, cols=None):
        self.arr, self.lead, self.r0, self.c0 = arr, lead, r0, c0
        self.rows = arr.shape[-2] - r0 if rows is None else rows
        self.cols = arr.shape[-1] - c0 if cols is None else cols

    def spec(self, br, bc):
        assert self.r0 % br == 0 and self.c0 % bc == 0 and self.rows % br == 0 and self.cols % bc == 0
        rb, cb, lead = self.r0 // br, self.c0 // bc, self.lead
        if lead is None:
            return pl.BlockSpec((br, bc), lambda i, j, k: (rb + k, cb + j))
        return pl.BlockSpec((None, br, bc), lambda i, j, k: (lead, rb + k, cb + j))


def _mm(name, a_parts, b_parts, epi, extras, outs, *, tm, tn, tk=None):
    b_parts = [b if isinstance(b, _W) else _W(b) for b in b_parts]
    t = a_parts[0].shape[0]
    n = b_parts[0].cols
    assert t % tm == 0 and n % tn == 0
    if tk is None:
        nk = 1
        a_specs = [pl.BlockSpec((tm, a.shape[1]), lambda i, j, k: (i, 0)) for a in a_parts]
        b_specs = [b.spec(b.rows, tn) for b in b_parts]
    else:
        assert len(a_parts) == 1 and a_parts[0].shape[1] % tk == 0
        nk = a_parts[0].shape[1] // tk
        a_specs = [pl.BlockSpec((tm, tk), lambda i, j, k: (i, k))]
        b_specs = [b_parts[0].spec(tk, tn)]
    e_specs = [pl.BlockSpec(bs, im) for (_, bs, im) in extras]
    out_shape = [jax.ShapeDtypeStruct((t, cols), dt) for (cols, _, dt) in outs]
    out_specs = [pl.BlockSpec((tm, bc), lambda i, j, k: (i, j)) for (_, bc, _) in outs]
    scratch = [pltpu.VMEM((tm, tn), F32)] if nk > 1 else []
    body = functools.partial(_mm_body, na=len(a_parts), ne=len(extras), no=len(outs), nk=nk,
                             epi=epi)
    res = pl.pallas_call(
        body,
        out_shape=out_shape,
        grid=(t // tm, n // tn, nk),
        in_specs=a_specs + b_specs + e_specs,
        out_specs=out_specs,
        scratch_shapes=scratch,
        compiler_params=_params(("parallel", "parallel", "arbitrary")),
        name=name,
    )(*a_parts, *[b.arr for b in b_parts], *[e[0] for e in extras])
    return res


def _epi_plain(acc, e_refs, o_refs):
    o_refs[0][...] = acc.astype(o_refs[0].dtype)


def _epi_relu2(acc, e_refs, o_refs):
    r = jnp.maximum(acc, 0.0)
    o_refs[0][...] = (r * r).astype(o_refs[0].dtype)


def _epi_resid(acc, e_refs, o_refs):
    o_refs[0][...] = e_refs[0][...] + acc


def _epi_rownorm(acc, e_refs, o_refs):
    o_refs[0][...] = (_rms(acc, acc.shape[-1]) * e_refs[0][...]).astype(o_refs[0].dtype)


def _epi_ckv(acc, e_refs, o_refs, *, rank):
    y = acc[:, :rank]
    o_refs[0][...] = (_rms(y, rank) * e_refs[0][...]).astype(o_refs[0].dtype)
    o_refs[1][...] = acc[:, rank:]


def _epi_mem(acc, e_refs, o_refs, *, heads, hd):
    g_ref, mk_ref, mv_ref = e_refs
    c = (hd ** -0.5) * LOG2E
    for h in range(heads):
        sl = slice(h * hd, (h + 1) * hd)
        q = (_rms(acc[:, sl], hd) * (g_ref[...] * c)).astype(BF16)
        s = lax.dot_general(q, mk_ref[0, :, sl], (((1,), (1,)), ((), ())),
                            preferred_element_type=F32)
        p = jnp.exp2(s - jnp.max(s, axis=-1, keepdims=True))
        l = jnp.sum(p, axis=-1, keepdims=True)
        o = jnp.dot(p.astype(BF16), mv_ref[0, :, sl], preferred_element_type=F32)
        o_refs[0][:, sl] = (o / l).astype(o_refs[0].dtype)


def _epi_headnorm(acc, e_refs, o_refs, *, hd):
    for h in range(acc.shape[-1] // hd):
        sl = slice(h * hd, (h + 1) * hd)
        o_refs[0][:, sl] = (_rms(acc[:, sl], hd) * e_refs[0][...]).astype(o_refs[0].dtype)


def _decay_lanes(pieces, head, sign_q):
    r = lax.broadcasted_iota(jnp.int32, (3 * LANES, LANES), 0)
    c = lax.broadcasted_iota(jnp.int32, (3 * LANES, LANES), 1)
    if sign_q:
        sel = jnp.where((c < 3) & (r == c * LANES + head), 1.0, 0.0)
    else:
        sel = jnp.where((c >= 3) & (c < 6) & (r == (c - 3) * LANES + head), -1.0, 0.0)
    aug = jnp.dot(pieces, sel.astype(BF16), preferred_element_type=F32)
    lane = lax.broadcasted_iota(jnp.int32, (1, LANES), 1)
    ones = jnp.where((lane >= 3) & (lane < 6), 1.0, 0.0) if sign_q else jnp.where(lane < 3, 1.0, 0.0)
    return aug + ones


def _epi_fox_qk(acc, e_refs, o_refs, *, hd, is_q):
    g_ref, dcs_ref = e_refs
    hpt = acc.shape[-1] // hd
    j = pl.program_id(1)
    c = (hd ** -0.5) * LOG2E if is_q else 1.0
    g = g_ref[...] * c
    pieces = jnp.concatenate(_split3(dcs_ref[...]), axis=-1)
    for h in range(hpt):
        y = _rms(acc[:, h * hd:(h + 1) * hd], hd) * g
        o_refs[0][:, h * HEAD_SLOT:h * HEAD_SLOT + hd] = y.astype(o_refs[0].dtype)
        aug = _decay_lanes(pieces, j * hpt + h, is_q)
        o_refs[0][:, h * HEAD_SLOT + hd:(h + 1) * HEAD_SLOT] = aug.astype(o_refs[0].dtype)


def _rope(u, c_ref, s1_ref, s2_ref, half):
    return (u * c_ref[...] + pltpu.roll(u, LANES - half, 1) * s1_ref[...]
            + pltpu.roll(u, half, 1) * s2_ref[...])


def _epi_mla_q(acc, e_refs, o_refs, *, nope, rope):
    g_ref, c_ref, s1_ref, s2_ref = e_refs
    qk = nope + rope
    c = (qk ** -0.5) * LOG2E
    g = g_ref[...] * c
    for h in range(acc.shape[-1] // HEAD_SLOT):
        y = _rms(acc[:, h * HEAD_SLOT:(h + 1) * HEAD_SLOT], qk) * g
        o_refs[0][:, h * HEAD_SLOT:h * HEAD_SLOT + nope] = y[:, :nope].astype(o_refs[0].dtype)
        u = _rope(y[:, nope:], c_ref, s1_ref, s2_ref, rope // 2)
        o_refs[0][:, h * HEAD_SLOT + nope:(h + 1) * HEAD_SLOT] = u.astype(o_refs[0].dtype)


def _epi_mla_k(acc, e_refs, o_refs, *, nope, rope):
    g_ref, kr_ref, c_ref, s1_ref, s2_ref = e_refs
    qk = nope + rope
    kr = kr_ref[...]
    ss_r = jnp.sum(kr * kr, axis=-1, keepdims=True)
    g = g_ref[...]
    kr_roped = _rope(kr * g[:, nope:], c_ref, s1_ref, s2_ref, rope // 2)
    for h in range(acc.shape[-1] // nope):
        y = acc[:, h * nope:(h + 1) * nope]
        ms = (jnp.sum(y * y, axis=-1, keepdims=True) + ss_r) * (1.0 / qk)
        r = lax.rsqrt(ms + NORM_EPS)
        o_refs[0][:, h * HEAD_SLOT:h * HEAD_SLOT + nope] = (y * r * g[:, :nope]).astype(
            o_refs[0].dtype)
        o_refs[0][:, h * HEAD_SLOT + nope:(h + 1) * HEAD_SLOT] = (kr_roped * r).astype(
            o_refs[0].dtype)


def _gate_body(f_ref, b_ref, o_ref, *, chunk):
    s = f_ref.shape[0]
    r = lax.broadcasted_iota(jnp.int32, (chunk, chunk), 0)
    c = lax.broadcasted_iota(jnp.int32, (chunk, chunk), 1)
    tri = jnp.where(r >= c, 1.0, 0.0).astype(BF16)
    carry = jnp.zeros((1, f_ref.shape[1]), F32)
    for i in range(s // chunk):
        z = f_ref[i * chunk:(i + 1) * chunk, :] + b_ref[...]
        lf = jnp.minimum(z, 0.0) - jnp.log1p(jnp.exp(-jnp.abs(z)))
        p1, p2, p3 = _split3(lf)
        cum = (jnp.dot(tri, p1, preferred_element_type=F32)
               + jnp.dot(tri, p2, preferred_element_type=F32)
               + jnp.dot(tri, p3, preferred_element_type=F32)) + carry
        o_ref[i * chunk:(i + 1) * chunk, :] = cum * LOG2E
        carry = cum[chunk - 1:chunk, :]


def _fox_gates(f_logit, b_f_row, seq):
    t, w = f_logit.shape
    chunk = _tile(seq, 256, 8)
    return pl.pallas_call(
        functools.partial(_gate_body, chunk=chunk),
        out_shape=jax.ShapeDtypeStruct((t, w), F32),
        grid=(t // seq,),
        in_specs=[pl.BlockSpec((seq, w), lambda b: (b, 0)),
                  pl.BlockSpec((1, w), lambda b: (0, 0))],
        out_specs=pl.BlockSpec((seq, w), lambda b: (b, 0)),
        compiler_params=_params(("parallel",)),
        name="fox_gates",
    )(f_logit, b_f_row)


def _rope_body(pos_ref, inv_ref, c_ref, s1_ref, s2_ref, *, half):
    ang = pos_ref[...].astype(F32) * inv_ref[...]
    cosv = jnp.cos(ang)
    sinv = jnp.sin(ang)
    lane = lax.broadcasted_iota(jnp.int32, ang.shape, 1)
    c_ref[...] = jnp.where(lane < 2 * half, cosv, 0.0)
    s1_ref[...] = jnp.where(lane < half, -sinv, 0.0)
    s2_ref[...] = jnp.where((lane >= half) & (lane < 2 * half), sinv, 0.0)


def _rope_tables(positions, rope):
    t = positions.size
    half = rope // 2
    inv_freq = 1.0 / (ROPE_THETA ** (jnp.arange(0, rope, 2, dtype=F32) / rope))
    inv = jnp.zeros((1, LANES), F32).at[0, :half].set(inv_freq).at[0, half:rope].set(inv_freq)
    tr = _tile(t, 512, 8)
    spec = pl.BlockSpec((tr, LANES), lambda i: (i, 0))
    shp = jax.ShapeDtypeStruct((t, LANES), F32)
    return pl.pallas_call(
        functools.partial(_rope_body, half=half),
        out_shape=[shp, shp, shp],
        grid=(t // tr,),
        in_specs=[pl.BlockSpec((tr, 1), lambda i: (i, 0)),
                  pl.BlockSpec((1, LANES), lambda i: (0, 0))],
        out_specs=[spec, spec, spec],
        compiler_params=_params(("parallel",)),
        name="rope_tables",
    )(positions.reshape(t, 1), inv)


def _flash_body(q_ref, k_ref, v_ref, o_ref, st_ref, *, tq, hp, dk, dv):
    qi = pl.program_id(2)

    def scores(h, j):
        start = pl.multiple_of(j * tq, tq)
        q = q_ref[:, h * dk:(h + 1) * dk]
        k = k_ref[pl.ds(start, tq), h * dk:(h + 1) * dk]
        st_ref[h] = lax.dot_general(k, q, (((1,), (1,)), ((), ())), preferred_element_type=F32)

    def step(j, stats, masked):
        start = pl.multiple_of(j * tq, tq)
        new = []
        for h in range(hp):
            m, l, acc = stats[h]
            if h + 1 < hp:
                scores(h + 1, j)
            elif not masked:
                scores(0, j + 1)
            st = st_ref[h]
            v = v_ref[pl.ds(start, tq), h * dv:(h + 1) * dv]
            if masked:
                key = lax.broadcasted_iota(jnp.int32, st.shape, 0)
                qry = lax.broadcasted_iota(jnp.int32, st.shape, 1)
                st = jnp.where(key <= qry, st, NEG_BIG)
            m_new = jnp.maximum(m, jnp.max(st, axis=0, keepdims=True))
            alpha = jnp.exp2(m - m_new)
            pt = jnp.exp2(st - m_new)
            l = alpha * l + jnp.sum(pt, axis=0, keepdims=True)
            pv = lax.dot_general(v, pt.astype(BF16), (((0,), (0,)), ((), ())),
                                 preferred_element_type=F32)
            new.append((m_new, l, alpha * acc + pv))
        return tuple(new)

    init = tuple((jnp.full((1, tq), NEG_BIG, F32), jnp.zeros((1, tq), F32),
                  jnp.zeros((dv, tq), F32)) for _ in range(hp))
    scores(0, 0)
    stats = lax.fori_loop(0, qi, functools.partial(step, masked=False), init)
    stats = step(qi, stats, True)
    for h in range(hp):
        _, l, acc = stats[h]
        o_ref[:, h * dv:(h + 1) * dv] = (acc / l).T.astype(o_ref.dtype)


def _flash(q, k, v, *, batch, seq, heads, tq=512, hp=6):
    t = q.shape[0]
    dv = v.shape[1] // heads
    tq = _tile(seq, tq, LANES)
    nq = seq // tq
    assert heads % hp == 0
    return pl.pallas_call(
        functools.partial(_flash_body, tq=tq, hp=hp, dk=HEAD_SLOT, dv=dv),
        out_shape=jax.ShapeDtypeStruct((t, heads * dv), BF16),
        grid=(batch, heads // hp, nq),
        in_specs=[pl.BlockSpec((tq, hp * HEAD_SLOT), lambda b, h, i: (b * nq + i, h)),
                  pl.BlockSpec((seq, hp * HEAD_SLOT), lambda b, h, i: (b, h)),
                  pl.BlockSpec((seq, hp * dv), lambda b, h, i: (b, h))],
        out_specs=pl.BlockSpec((tq, hp * dv), lambda b, h, i: (b * nq + i, h)),
        scratch_shapes=[pltpu.VMEM((hp, tq, tq), F32)],
        compiler_params=_params(("parallel", "parallel", "arbitrary")),
        name="flash",
    )(q, k, v)


def _bf(w):
    return w.astype(BF16)


def _row(g):
    return g.reshape(1, -1).astype(F32)


def _const_spec(shape):
    return (shape, lambda i, j, k: (0,) * len(shape))


def _mem_branch(name, h, w_qmem, memq_g, mem_k, mem_v, *, seq, tm):
    width = w_qmem.shape[1]
    hd = memq_g.shape[0]
    mlen = mem_k.shape[1]
    per_b = seq // tm
    extras = [(_row(memq_g),) + _const_spec((1, hd)),
              (mem_k, (1, mlen, width), lambda i, j, k: (i // per_b, 0, 0)),
              (mem_v, (1, mlen, width), lambda i, j, k: (i // per_b, 0, 0))]
    epi = functools.partial(_epi_mem, heads=width // hd, hd=hd)
    return _mm(name, [h], [_bf(w_qmem)], epi, extras, [(width, width, BF16)], tm=tm, tn=width)[0]


def _out_and_mlp(x, o_self, o_mem, layer, w_o, mlp_g, w_up, w_down, *, tm):
    t, d = x.shape
    sw = o_self.shape[1]
    tn = _tile(d, 512)
    x = _mm("attn_out", [o_self, o_mem],
            [_W(w_o, layer, 0, sw), _W(w_o, layer, sw, d - sw)], _epi_resid,
            [(x, (tm, tn), lambda i, j, k: (i, j))], [(d, tn, F32)], tm=tm, tn=tn)[0]
    h = _rmsnorm(x, mlp_g)
    dff = w_up.shape[-1]
    hid = _mm("mlp_up", [h], [_W(w_up, layer)], _epi_relu2, [], [(dff, _tile(dff, 1024), BF16)],
              tm=tm, tn=_tile(dff, 1024))[0]
    tn = _tile(d, 1024)
    x = _mm("mlp_down", [hid], [_W(w_down, layer)], _epi_resid,
            [(x, (tm, tn), lambda i, j, k: (i, j))], [(d, tn, F32)],
            tm=tm, tn=tn, tk=_tile(dff, 2048))[0]
    return x


def _fox_layer(x, h, w_in_all, j, b_f, q_g, k_g, memq_g, mem_k, mem_v, *, batch, seq, tm):
    t, d = x.shape
    heads = b_f.shape[0]
    hd = q_g.shape[0]
    sw = heads * hd
    assert hd == LANES
    w_q, w_k, w_v = (_W(w_in_all, j, 0, d, i * sw, sw) for i in range(3))
    w_in = w_in_all[j]
    w_f = jnp.zeros((d, LANES), BF16).at[:, :heads].set(w_in[:, 3 * sw:3 * sw + heads])
    w_qmem = w_in[:, 3 * sw + heads:]

    f_logit = _mm("fox_f", [h], [w_f], _epi_plain, [], [(LANES, LANES, F32)], tm=tm, tn=LANES)[0]
    b_row = jnp.zeros((1, LANES), F32).at[0, :heads].set(b_f)
    dcs = _fox_gates(f_logit, b_row, seq)

    tn = _tile(sw, 1024)
    slot_cols = heads * HEAD_SLOT
    dcs_extra = (dcs, (tm, LANES), lambda i, j, k: (i, 0))
    qk = []
    for name, w, g, is_q in (("fox_q", w_q, q_g, True), ("fox_k", w_k, k_g, False)):
        epi = functools.partial(_epi_fox_qk, hd=hd, is_q=is_q)
        extras = [(_row(g),) + _const_spec((1, hd)), dcs_extra]
        qk.append(_mm(name, [h], [w], epi, extras,
                      [(slot_cols, tn // hd * HEAD_SLOT, BF16)], tm=tm, tn=tn)[0])
    v = _mm("fox_v", [h], [w_v], _epi_plain, [], [(sw, tn, BF16)], tm=tm, tn=tn)[0]
    o_self = _flash(qk[0], qk[1], v, batch=batch, seq=seq, heads=heads)
    o_mem = _mem_branch("fox_mem", h, w_qmem, memq_g, mem_k, mem_v, seq=seq, tm=tm)
    return o_self, o_mem


def _mla_layer(x, h, tables, w_in_all, j, q_a_g, w_q_b, kv_a_g, w_kv_b, q_g, k_g, memq_g, mem_k,
               mem_v, *, batch, seq, heads, tm):
    t, d = x.shape
    w_in = w_in_all[j]
    q_rank = q_a_g.shape[0]
    kv_rank = kv_a_g.shape[0]
    qk = q_g.shape[0]
    vd = (d - mem_k.shape[2]) // heads
    nope = w_kv_b.shape[1] // heads - vd
    rope = qk - nope
    assert nope == LANES and rope <= LANES and vd % LANES == 0
    c_tab, s1_tab, s2_tab = tables
    tab_extras = [(tab, (tm, LANES), lambda i, j, k: (i, 0)) for tab in (c_tab, s1_tab, s2_tab)]

    tm_q = _tile(seq, 512, 8)
    c_q = _mm("mla_cq", [h], [_W(w_in_all, j, 0, d, 0, q_rank)], _epi_rownorm,
              [(_row(q_a_g),) + _const_spec((1, q_rank))], [(q_rank, q_rank, BF16)],
              tm=tm_q, tn=q_rank)[0]
    w_ckv = jnp.zeros((d, kv_rank + LANES), BF16).at[:, :kv_rank + rope].set(
        _bf(w_in[:, q_rank:q_rank + kv_rank + rope]))
    c_kv, k_rope = _mm("mla_ckv", [h], [w_ckv], functools.partial(_epi_ckv, rank=kv_rank),
                       [(_row(kv_a_g),) + _const_spec((1, kv_rank))],
                       [(kv_rank, kv_rank, BF16), (LANES, LANES, F32)],
                       tm=tm, tn=kv_rank + LANES)
    w_qmem = w_in[:, q_rank + kv_rank + rope:]

    w_q = jnp.zeros((q_rank, heads, HEAD_SLOT), BF16).at[:, :, :qk].set(
        _bf(w_q_b).reshape(q_rank, heads, qk)).reshape(q_rank, heads * HEAD_SLOT)
    g_q = jnp.zeros((1, HEAD_SLOT), F32).at[0, :qk].set(q_g)
    tn = _tile(heads * HEAD_SLOT, 1024, HEAD_SLOT)
    q = _mm("mla_q", [c_q], [w_q], functools.partial(_epi_mla_q, nope=nope, rope=rope),
            [(g_q,) + _const_spec((1, HEAD_SLOT))] + tab_extras,
            [(heads * HEAD_SLOT, tn, BF16)], tm=tm, tn=tn)[0]

    w_kv = _bf(w_kv_b).reshape(kv_rank, heads, nope + vd)
    w_kn = w_kv[:, :, :nope].reshape(kv_rank, heads * nope)
    w_v = w_kv[:, :, nope:].reshape(kv_rank, heads * vd)
    g_k = jnp.zeros((1, nope + LANES), F32).at[0, :qk].set(k_g)
    tn = _tile(heads * nope, 1024)
    k = _mm("mla_k", [c_kv], [w_kn], functools.partial(_epi_mla_k, nope=nope, rope=rope),
            [(g_k,) + _const_spec((1, nope + LANES)),
             (k_rope, (tm, LANES), lambda i, j, k: (i, 0))] + tab_extras,
            [(heads * HEAD_SLOT, tn // nope * HEAD_SLOT, BF16)], tm=tm, tn=tn)[0]
    tn = _tile(heads * vd, 1024)
    v = _mm("mla_v", [c_kv], [w_v], _epi_plain, [], [(heads * vd, tn, BF16)], tm=tm, tn=tn)[0]

    o_self = _flash(q, k, v, batch=batch, seq=seq, heads=heads)
    o_mem = _mem_branch("mla_mem", h, w_qmem, memq_g, mem_k, mem_v, seq=seq, tm=tm)
    return o_self, o_mem


def kernel(x, mem, positions, mem_norm_g, w_mem_kv, mem_k_norm_g, attn_norm_g, memq_norm_g, w_o,
           mlp_norm_g, w_up, w_down, fox_w_in, fox_b_f, fox_q_norm_g, fox_k_norm_g, mla_w_in,
           mla_q_a_norm_g, mla_w_q_b, mla_kv_a_norm_g, mla_w_kv_b, mla_q_norm_g, mla_k_norm_g):
    batch, seq, d = x.shape
    mlen = mem.shape[1]
    mem_hd = mem_k_norm_g.shape[0]
    mem_w = w_mem_kv.shape[1] // 2
    heads = fox_b_f.shape[1]
    depth = attn_norm_g.shape[0]
    tm = _tile(seq, 1024, 8)

    mem_n = _rmsnorm(mem.reshape(batch * mlen, d), mem_norm_g)
    mem_k = _mm("mem_k", [mem_n], [_bf(w_mem_kv[:, :mem_w])],
                functools.partial(_epi_headnorm, hd=mem_hd),
                [(_row(mem_k_norm_g),) + _const_spec((1, mem_hd))], [(mem_w, mem_w, BF16)],
                tm=mlen, tn=mem_w)[0].reshape(batch, mlen, mem_w)
    mem_v = _mm("mem_v", [mem_n], [_bf(w_mem_kv[:, mem_w:])], _epi_plain, [],
                [(mem_w, mem_w, BF16)], tm=mlen, tn=mem_w)[0].reshape(batch, mlen, mem_w)

    rope = mla_q_norm_g.shape[1] - (mla_w_kv_b.shape[2] // heads - (d - mem_w) // heads)
    tables = _rope_tables(positions, rope)

    fox_w, mla_w, w_o, w_up, w_down = (_bf(w) for w in (fox_w_in, mla_w_in, w_o, w_up, w_down))
    x = x.reshape(batch * seq, d)
    for layer in range(depth):
        j = layer // 2
        h = _rmsnorm(x, attn_norm_g[layer])
        if layer % 2 == 0:
            o_self, o_mem = _fox_layer(x, h, fox_w, j, fox_b_f[j], fox_q_norm_g[j],
                                       fox_k_norm_g[j], memq_norm_g[layer], mem_k, mem_v,
                                       batch=batch, seq=seq, tm=tm)
        else:
            o_self, o_mem = _mla_layer(x, h, tables, mla_w, j, mla_q_a_norm_g[j], mla_w_q_b[j],
                                       mla_kv_a_norm_g[j], mla_w_kv_b[j], mla_q_norm_g[j],
                                       mla_k_norm_g[j], memq_norm_g[layer], mem_k, mem_v,
                                       batch=batch, seq=seq, heads=heads, tm=tm)
        x = _out_and_mlp(x, o_self, o_mem, layer, w_o, mlp_norm_g[layer], w_up, w_down, tm=tm)
    return x.reshape(batch, seq, d)
```
